```python
import math
import jax, jax.numpy as jnp
from jax import lax
import numpy as np

D_MODEL = 1024
BATCH = 16
SEQ = 4096
DEPTH = 4
DEC_BATCH = 8
DEC_SEQ = 4096
PAST_LEN = 128

HEAD_DIM = 64
A_Q_HEADS = D_MODEL // 128
A_KV_HEADS = A_Q_HEADS // 4
B_GROUPS = ((128, 1), (512, 4), (2048, 16))
B_HEADS_PER_GROUP = D_MODEL // 256
B_HEADS = B_HEADS_PER_GROUP * len(B_GROUPS)
A_WIDTH = A_Q_HEADS * HEAD_DIM
A_KV_WIDTH = A_KV_HEADS * HEAD_DIM
B_QKV_WIDTH = B_HEADS * HEAD_DIM
B_WIDTH = B_HEADS_PER_GROUP * HEAD_DIM
PLE_DIM = 256
GRID_W = 64
ROPE_THETA = 10000.0
Q_BLOCK = 128
EPS = 1e-6
IN_SIZES = (A_WIDTH, A_KV_WIDTH, A_KV_WIDTH, A_WIDTH,
            B_QKV_WIDTH, B_QKV_WIDTH, B_QKV_WIDTH, B_WIDTH,
            D_MODEL, D_MODEL)
IN_WIDTH = sum(IN_SIZES)

kernel_name = "hybrid_axial_gqa_dilated_gated_encoder"


def rmsnorm(x, g):
    xf = x.astype(jnp.float32)
    y = xf * lax.rsqrt(jnp.mean(xf * xf, axis=-1, keepdims=True) + EPS)
    return (y * g.astype(jnp.float32)).astype(x.dtype)


def rope_table(pos, dim):
    freqs = ROPE_THETA ** (-jnp.arange(0, dim, 2, dtype=jnp.float32) / dim)
    ang = pos.astype(jnp.float32)[:, None] * freqs[None, :]
    ang = jnp.concatenate([ang, ang], axis=-1)
    return jnp.cos(ang), jnp.sin(ang)


def apply_rope(x, cos, sin):
    half = x.shape[-1] // 2
    x1, x2 = x[..., :half], x[..., half:]
    rot = jnp.concatenate([-x2, x1], axis=-1)
    y = x * cos[None, :, None, :] + rot * sin[None, :, None, :]
    return y.astype(x.dtype)


def split_points():
    pts, acc = [], 0
    for s in IN_SIZES[:-1]:
        acc += s
        pts.append(acc)
    return pts


def gqa_attention(q, k, v):
    bsz, s_len, hq, dh = q.shape
    hkv = k.shape[2]
    g = hq // hkv
    nblk = s_len // Q_BLOCK
    scale = 1.0 / math.sqrt(dh)
    qb = q.reshape(bsz, nblk, Q_BLOCK, hkv, g, dh).transpose(1, 0, 2, 3, 4, 5)

    def blk(qblk):
        s = jnp.einsum('bqkgd,bskd->bkgqs', qblk, k).astype(jnp.float32) * scale
        p = jax.nn.softmax(s, axis=-1).astype(v.dtype)
        return jnp.einsum('bkgqs,bskd->bqkgd', p, v)

    o = lax.map(blk, qb)
    return o.transpose(1, 0, 2, 3, 4, 5).reshape(bsz, s_len, hq * dh)


def dilated_group(q, k, v, window, dilation):
    bsz, s_len, h, dh = q.shape
    n_side = (window // 2) // dilation
    offs = dilation * jnp.arange(-n_side, n_side + 1)
    nblk = s_len // Q_BLOCK
    scale = 1.0 / math.sqrt(dh)
    qb = q.reshape(bsz, nblk, Q_BLOCK, h, dh).transpose(1, 0, 2, 3, 4)
    starts = jnp.arange(nblk) * Q_BLOCK

    def blk(args):
        qblk, s0 = args
        kpos = s0 + jnp.arange(Q_BLOCK)[:, None] + offs[None, :]
        valid = (kpos >= 0) & (kpos < s_len)
        kidx = jnp.clip(kpos, 0, s_len - 1)
        kg = jnp.take(k, kidx, axis=1)
        vg = jnp.take(v, kidx, axis=1)
        s = jnp.einsum('bqhd,bqjhd->bhqj', qblk, kg).astype(jnp.float32) * scale
        s = jnp.where(valid[None, None], s, -jnp.inf)
        lse = jax.nn.logsumexp(s, axis=-1)
        p = jnp.exp(s - lse[..., None]).astype(v.dtype)
        o = jnp.einsum('bhqj,bqjhd->bqhd', p, vg)
        return o, lse.transpose(0, 2, 1)

    o, lse = lax.map(blk, (qb, starts))
    o = o.transpose(1, 0, 2, 3, 4).reshape(bsz, s_len, h, dh)
    lse = lse.transpose(1, 0, 2, 3).reshape(bsz, s_len, h)
    return o, lse


def trunk(x, p, g_norm, w_in, g_q, g_k, w_a, w_b, w_o, w_ple, g_ple, w_pg, g_final):
    bsz, s_len, _ = x.shape
    rows = s_len // GRID_W
    r_idx, c_idx = jnp.meshgrid(jnp.arange(rows), jnp.arange(GRID_W), indexing='ij')
    cos_r, sin_r = rope_table(r_idx.reshape(-1), HEAD_DIM // 2)
    cos_c, sin_c = rope_table(c_idx.reshape(-1), HEAD_DIM // 2)
    cos_t, sin_t = rope_table(jnp.arange(s_len), HEAD_DIM)
    pts = split_points()
    half = HEAD_DIM // 2
    h = x
    for i in range(DEPTH):
        u = rmsnorm(h, g_norm[i])
        proj = u @ w_in[i]
        qa, ka, va, za, qb, kb, vb, zb, ga, gb = jnp.split(proj, pts, axis=-1)

        qa = rmsnorm(qa.reshape(bsz, s_len, A_Q_HEADS, HEAD_DIM), g_q[i])
        ka = rmsnorm(ka.reshape(bsz, s_len, A_KV_HEADS, HEAD_DIM), g_k[i])
        va = va.reshape(bsz, s_len, A_KV_HEADS, HEAD_DIM)
        qa = jnp.concatenate([apply_rope(qa[..., :half], cos_r, sin_r),
                              apply_rope(qa[..., half:], cos_c, sin_c)], axis=-1)
        ka = jnp.concatenate([apply_rope(ka[..., :half], cos_r, sin_r),
                              apply_rope(ka[..., half:], cos_c, sin_c)], axis=-1)
        ya = gqa_attention(qa, ka, va) * jax.nn.silu(za)
        ya = ya @ w_a[i]

        qb = apply_rope(qb.reshape(bsz, s_len, B_HEADS, HEAD_DIM), cos_t, sin_t)
        kb = apply_rope(kb.reshape(bsz, s_len, B_HEADS, HEAD_DIM), cos_t, sin_t)
        vb = vb.reshape(bsz, s_len, B_HEADS, HEAD_DIM)
        outs, lses = [], []
        for gi, (window, dilation) in enumerate(B_GROUPS):
            sl = slice(gi * B_HEADS_PER_GROUP, (gi + 1) * B_HEADS_PER_GROUP)
            o_g, lse_g = dilated_group(qb[:, :, sl], kb[:, :, sl], vb[:, :, sl], window, dilation)
            outs.append(o_g)
            lses.append(lse_g)
        wts = jax.nn.softmax(jnp.stack(lses, axis=0), axis=0)
        ob = jnp.sum(wts[..., None].astype(vb.dtype) * jnp.stack(outs, axis=0), axis=0)
        yb = ob.reshape(bsz, s_len, B_WIDTH) * jax.nn.silu(zb)
        yb = yb @ w_b[i]

        merged = jax.nn.sigmoid(ga) * ya + jax.nn.sigmoid(gb) * yb
        h = h + merged @ w_o[i]

        e = p[i] @ w_ple[i]
        gate = jax.nn.sigmoid(rmsnorm(h, g_ple[i]) @ w_pg[i])
        h = h + gate * e
    return rmsnorm(h, g_final)


def setup_inputs(seed: int = 0) -> dict:
    key = jax.random.key(seed)
    ks = jax.random.split(key, 16)
    f32 = jnp.float32

    def nrm(k, shape, fan_in):
        return jax.random.normal(k, shape, f32) * (fan_in ** -0.5)

    return {
        "x_prompt": jax.random.normal(ks[0], (BATCH, SEQ, D_MODEL), f32),
        "x_sample": jax.random.normal(ks[1], (DEC_BATCH, DEC_SEQ, D_MODEL), f32),
        "p_prompt": jax.random.normal(ks[2], (DEPTH, BATCH, SEQ, PLE_DIM), f32),
        "p_sample": jax.random.normal(ks[3], (DEPTH, DEC_BATCH, DEC_SEQ, PLE_DIM), f32),
        "g_norm": 1.0 + 0.02 * jax.random.normal(ks[4], (DEPTH, D_MODEL), f32),
        "w_in": nrm(ks[5], (DEPTH, D_MODEL, IN_WIDTH), D_MODEL),
        "g_q": 1.0 + 0.02 * jax.random.normal(ks[6], (DEPTH, HEAD_DIM), f32),
        "g_k": 1.0 + 0.02 * jax.random.normal(ks[7], (DEPTH, HEAD_DIM), f32),
        "w_a": nrm(ks[8], (DEPTH, A_WIDTH, D_MODEL), A_WIDTH),
        "w_b": nrm(ks[9], (DEPTH, B_WIDTH, D_MODEL), B_WIDTH),
        "w_o": nrm(ks[10], (DEPTH, D_MODEL, D_MODEL), D_MODEL),
        "w_ple": nrm(ks[11], (DEPTH, PLE_DIM, D_MODEL), PLE_DIM),
        "g_ple": 1.0 + 0.02 * jax.random.normal(ks[12], (DEPTH, D_MODEL), f32),
        "w_pg": nrm(ks[13], (DEPTH, D_MODEL, D_MODEL), D_MODEL),
        "g_final": 1.0 + 0.02 * jax.random.normal(ks[14], (D_MODEL,), f32),
    }


def reference(x_prompt, x_sample, p_prompt, p_sample, g_norm, w_in, g_q, g_k, w_a, w_b, w_o,
              w_ple, g_ple, w_pg, g_final):
    y_prompt = trunk(x_prompt, p_prompt, g_norm, w_in, g_q, g_k, w_a, w_b, w_o,
                     w_ple, g_ple, w_pg, g_final)
    y_sample = trunk(x_sample, p_sample, g_norm, w_in, g_q, g_k, w_a, w_b, w_o,
                     w_ple, g_ple, w_pg, g_final)
    return (y_prompt, y_sample)
```

```python
import functools
import math

import jax
import jax.numpy as jnp
from jax import lax
from jax.experimental import pallas as pl
from jax.experimental.pallas import tpu as pltpu

D_MODEL = 1024
DEPTH = 4
HEAD_DIM = 64
A_Q_HEADS = 8
A_KV_HEADS = 2
A_GROUP = A_Q_HEADS // A_KV_HEADS
B_GROUPS = ((128, 1), (512, 4), (2048, 16))
B_HEADS_PER_GROUP = 4
A_WIDTH = A_Q_HEADS * HEAD_DIM
A_KV_WIDTH = A_KV_HEADS * HEAD_DIM
B_WIDTH = B_HEADS_PER_GROUP * HEAD_DIM
B_QKV_WIDTH = B_WIDTH * len(B_GROUPS)
PLE_DIM = 256
GRID_W = 64
ROPE_THETA = 10000.0
EPS = 1e-6
SCALE = 1.0 / math.sqrt(HEAD_DIM)

_OFF_QA, _OFF_KA, _OFF_VA, _OFF_ZA = 0, 512, 640, 768
_OFF_QB, _OFF_KB, _OFF_VB, _OFF_ZB = 1280, 2048, 2816, 3584
_OFF_GA, _OFF_GB, _IN_WIDTH = 3840, 4864, 5888
QKV_WIDTH = A_WIDTH + 2 * A_KV_WIDTH + 3 * B_QKV_WIDTH
GATE_WIDTH = A_WIDTH + B_WIDTH + 2 * D_MODEL

LANES = 128
VMEM_LIMIT = 56 * 1024 * 1024

TM_QKV = 512
TQ_A = 128
TK_A = 512
TU_B = 128
B_SIDE = 64
TM_OUT = 256

F32 = jnp.float32
BF16 = jnp.bfloat16


def _rmsnorm(x, g):
    return x * lax.rsqrt(jnp.mean(x * x, axis=-1, keepdims=True) + EPS) * g


def _rope(x, c, s_up, s_dn, half):
    return x * c + pltpu.roll(x, LANES - half, 1) * s_up + pltpu.roll(x, half, 1) * s_dn


def _head_meansq(x, seg):
    x2 = x * x
    hi = x2.astype(BF16)
    lo = (x2 - hi.astype(F32)).astype(BF16)
    return jnp.dot(jnp.concatenate([hi, lo], axis=-1), seg, preferred_element_type=F32)


def _qkv_kernel(h_ref, gn_ref, w_ref, gq_ref, gk_ref, seg_ref,
                ca_ref, sa1_ref, sa2_ref, cb_ref, sb1_ref, sb2_ref,
                qa_ref, kd_ref, vt_ref, qb_ref, kb_ref, vb_ref):
    u = _rmsnorm(h_ref[0], gn_ref[0]).astype(BF16)
    seg = seg_ref[...]
    ca, sa1, sa2 = ca_ref[...], sa1_ref[...], sa2_ref[...]
    cb, sb1, sb2 = cb_ref[...], sb1_ref[...], sb2_ref[...]
    lane = lax.broadcasted_iota(jnp.int32, ca.shape, 1)

    def norm_rope_a(x, g):
        xn = x * lax.rsqrt(_head_meansq(x, seg) + EPS) * g
        return _rope(xn, ca, sa1, sa2, HEAD_DIM // 4)

    qa = jnp.dot(u, w_ref[0, :, 0:A_WIDTH], preferred_element_type=F32)
    for blk in range(A_WIDTH // LANES):
        sl = slice(blk * LANES, (blk + 1) * LANES)
        qa_ref[0, :, sl] = (norm_rope_a(qa[:, sl], gq_ref[0]) * SCALE).astype(BF16)
    kv = jnp.dot(u, w_ref[0, :, A_WIDTH:A_WIDTH + 2 * A_KV_WIDTH], preferred_element_type=F32)
    ka = norm_rope_a(kv[:, 0:LANES], gk_ref[0])
    ka_sw = pltpu.roll(ka, HEAD_DIM, 1)
    kd_ref[0, 0] = jnp.where(lane < HEAD_DIM, ka, ka_sw).astype(BF16)
    kd_ref[0, 1] = jnp.where(lane < HEAD_DIM, ka_sw, ka).astype(BF16)
    vt_ref[0] = kv[:, LANES:2 * LANES].T.astype(BF16)

    base = A_WIDTH + 2 * A_KV_WIDTH
    qb = jnp.dot(u, w_ref[0, :, base:base + B_QKV_WIDTH], preferred_element_type=F32)
    for blk in range(B_QKV_WIDTH // LANES):
        sl = slice(blk * LANES, (blk + 1) * LANES)
        qb_ref[0, :, sl] = (_rope(qb[:, sl], cb, sb1, sb2, HEAD_DIM // 2) * SCALE).astype(BF16)
    kb = jnp.dot(u, w_ref[0, :, base + B_QKV_WIDTH:base + 2 * B_QKV_WIDTH],
                 preferred_element_type=F32)
    for blk in range(B_QKV_WIDTH // LANES):
        sl = slice(blk * LANES, (blk + 1) * LANES)
        kb_ref[0, :, sl] = _rope(kb[:, sl], cb, sb1, sb2, HEAD_DIM // 2).astype(BF16)
    vb_ref[0] = jnp.dot(u, w_ref[0, :, base + 2 * B_QKV_WIDTH:base + 3 * B_QKV_WIDTH],
                        preferred_element_type=F32).astype(BF16)


def _qkv_call(h, layer, w_qkv, g_norm, gq, gk, seg, tabs):
    bsz, s_len, _ = h.shape
    tm = TM_QKV
    nt = s_len // tm
    row = lambda b, i: (0, 0)
    tab_spec = pl.BlockSpec((tm, LANES), lambda b, i: (i, 0))
    vec_spec = lambda n: pl.BlockSpec((1, 1, n), lambda b, i: (layer, 0, 0))
    in_specs = [
        pl.BlockSpec((1, tm, D_MODEL), lambda b, i: (b, i, 0)),
        vec_spec(D_MODEL),
        pl.BlockSpec((1, D_MODEL, QKV_WIDTH), lambda b, i: (layer, 0, 0)),
        vec_spec(LANES), vec_spec(LANES),
        pl.BlockSpec((2 * LANES, LANES), row),
    ] + [tab_spec] * 6
    out_shape = [
        jax.ShapeDtypeStruct((bsz, s_len, A_WIDTH), BF16),
        jax.ShapeDtypeStruct((bsz, A_KV_HEADS, s_len, LANES), BF16),
        jax.ShapeDtypeStruct((bsz, A_KV_WIDTH, s_len), BF16),
        jax.ShapeDtypeStruct((bsz, s_len, B_QKV_WIDTH), BF16),
        jax.ShapeDtypeStruct((bsz, s_len, B_QKV_WIDTH), BF16),
        jax.ShapeDtypeStruct((bsz, s_len, B_QKV_WIDTH), BF16),
    ]
    out_specs = [
        pl.BlockSpec((1, tm, A_WIDTH), lambda b, i: (b, i, 0)),
        pl.BlockSpec((1, A_KV_HEADS, tm, LANES), lambda b, i: (b, 0, i, 0)),
        pl.BlockSpec((1, A_KV_WIDTH, tm), lambda b, i: (b, 0, i)),
        pl.BlockSpec((1, tm, B_QKV_WIDTH), lambda b, i: (b, i, 0)),
        pl.BlockSpec((1, tm, B_QKV_WIDTH), lambda b, i: (b, i, 0)),
        pl.BlockSpec((1, tm, B_QKV_WIDTH), lambda b, i: (b, i, 0)),
    ]
    return pl.pallas_call(
        _qkv_kernel,
        grid=(bsz, nt),
        in_specs=in_specs,
        out_specs=out_specs,
        out_shape=out_shape,
        compiler_params=pltpu.CompilerParams(
            dimension_semantics=("parallel", "parallel"), vmem_limit_bytes=VMEM_LIMIT),
        name="qkv",
    )(h, g_norm, w_qkv, gq, gk, seg, *tabs)


def _attn_a_kernel(q_ref, kd_ref, vt_ref, o_ref, *, s_len):
    tq, tk = TQ_A, TK_A
    nq = A_GROUP * tq
    lane = lax.broadcasted_iota(jnp.int32, (tq, LANES), 1)
    for j in range(A_KV_HEADS):
        parts = []
        for p in range(A_GROUP // 2):
            c0 = (j * A_GROUP + 2 * p) * HEAD_DIM
            blk = q_ref[0, :, c0:c0 + LANES].astype(F32)
            parts.append(jnp.where(lane < HEAD_DIM, blk, 0.0).astype(BF16))
            parts.append(jnp.where(lane < HEAD_DIM, 0.0, blk).astype(BF16))
        qst = jnp.concatenate(parts, axis=0)

        def body(c, carry):
            m, l, acc = carry
            off = pl.multiple_of(c * tk, tk)
            kc = kd_ref[0, j, pl.ds(off, tk), :]
            st = lax.dot_general(kc, qst, (((1,), (1,)), ((), ())),
                                 preferred_element_type=F32)
            m_new = jnp.maximum(m, jnp.max(st, axis=0, keepdims=True))
            alpha = jnp.exp(m - m_new)
            p_t = jnp.exp(st - m_new)
            l_new = alpha * l + jnp.sum(p_t, axis=0, keepdims=True)
            vc = vt_ref[0, j * HEAD_DIM:(j + 1) * HEAD_DIM, pl.ds(off, tk)]
            acc_new = alpha * acc + jnp.dot(vc, p_t.astype(BF16), preferred_element_type=F32)
            return m_new, l_new, acc_new

        init = (jnp.full((1, nq), -jnp.inf, F32), jnp.zeros((1, nq), F32),
                jnp.zeros((HEAD_DIM, nq), F32))
        _, l, acc = lax.fori_loop(0, s_len // tk, body, init)
        o_t = acc / l
        for p in range(A_GROUP // 2):
            pair = jnp.concatenate(
                [o_t[:, (2 * p) * tq:(2 * p + 1) * tq].T,
                 o_t[:, (2 * p + 1) * tq:(2 * p + 2) * tq].T], axis=1)
            c0 = (j * A_GROUP + 2 * p) * HEAD_DIM
            o_ref[0, :, c0:c0 + LANES] = pair


def _attn_a_call(qa, kd, vt):
    bsz, s_len, _ = qa.shape
    return pl.pallas_call(
        functools.partial(_attn_a_kernel, s_len=s_len),
        grid=(bsz, s_len // TQ_A),
        in_specs=[
            pl.BlockSpec((1, TQ_A, A_WIDTH), lambda b, i: (b, i, 0)),
            pl.BlockSpec((1, A_KV_HEADS, s_len, LANES), lambda b, i: (b, 0, 0, 0)),
            pl.BlockSpec((1, A_KV_WIDTH, s_len), lambda b, i: (b, 0, 0)),
        ],
        out_specs=pl.BlockSpec((1, TQ_A, A_WIDTH), lambda b, i: (b, i, 0)),
        out_shape=jax.ShapeDtypeStruct((bsz, s_len, A_WIDTH), F32),
        compiler_params=pltpu.CompilerParams(
            dimension_semantics=("parallel", "arbitrary"), vmem_limit_bytes=VMEM_LIMIT),
        name="attn_a",
    )(qa, kd, vt)


def _attn_b_kernel(q_ref, k_ref, v_ref, o_ref, lse_ref, *, s_class):
    tu = TU_B
    win = tu + 2 * B_SIDE
    u0 = pl.program_id(2) * tu
    start = pl.multiple_of(jnp.clip(u0 - B_SIDE, 0, s_class - win), B_SIDE)
    qpos = u0 + lax.broadcasted_iota(jnp.int32, (tu, win), 0)
    kpos = start + lax.broadcasted_iota(jnp.int32, (tu, win), 1)
    valid = jnp.abs(kpos - qpos) <= B_SIDE
    lane = lax.broadcasted_iota(jnp.int32, (tu, LANES), 1)
    low = lane < HEAD_DIM
    for p in range(B_HEADS_PER_GROUP // 2):
        sl = slice(p * LANES, (p + 1) * LANES)
        qblk = q_ref[0, :, sl].astype(F32)
        kp = k_ref[0, pl.ds(start, win), sl]
        vp = v_ref[0, pl.ds(start, win), sl]
        outs, lses = [], []
        for keep_low in (True, False):
            qm = jnp.where(low == keep_low, qblk, 0.0).astype(BF16)
            s = lax.dot_general(qm, kp, (((1,), (1,)), ((), ())), preferred_element_type=F32)
            s = jnp.where(valid, s, -jnp.inf)
            m = jnp.max(s, axis=-1, keepdims=True)
            e = jnp.exp(s - m)
            l = jnp.sum(e, axis=-1, keepdims=True)
            outs.append(jnp.dot(e.astype(BF16), vp, preferred_element_type=F32) / l)
            lses.append(m + jnp.log(l))
        o_ref[0, :, sl] = jnp.where(low, outs[0], outs[1])
        lse_ref[0, :, sl] = jnp.where(low, lses[0], lses[1])


def _attn_b_call(qb, kb, vb, group):
    bsz, s_len, _ = qb.shape
    _, dil = B_GROUPS[group]
    n_grp = len(B_GROUPS)
    s_class = s_len // dil
    view = lambda x: x.reshape(bsz, s_class, dil * B_QKV_WIDTH)
    q_spec = pl.BlockSpec((1, TU_B, B_WIDTH), lambda b, r, u: (b, u, r * n_grp + group))
    kv_spec = pl.BlockSpec((1, s_class, B_WIDTH), lambda b, r, u: (b, 0, r * n_grp + group))
    o_spec = pl.BlockSpec((1, TU_B, B_WIDTH), lambda b, r, u: (b, u, r))
    o_shape = jax.ShapeDtypeStruct((bsz, s_class, dil * B_WIDTH), F32)
    o, lse = pl.pallas_call(
        functools.partial(_attn_b_kernel, s_class=s_class),
        grid=(bsz, dil, s_class // TU_B),
        in_specs=[q_spec, kv_spec, kv_spec],
        out_specs=[o_spec, o_spec],
        out_shape=[o_shape, o_shape],
        compiler_params=pltpu.CompilerParams(
            dimension_semantics=("parallel", "parallel", "arbitrary"),
            vmem_limit_bytes=VMEM_LIMIT),
        name=f"attn_b{group}",
    )(view(qb), view(kb), view(vb))
    return o.reshape(bsz, s_len, B_WIDTH), lse.reshape(bsz, s_len, B_WIDTH)


def _out_kernel(h_ref, oa_ref, o1_ref, o2_ref, o3_ref, l1_ref, l2_ref, l3_ref, p_ref,
                gn_ref, wg_ref, wa_ref, wb_ref, wo_ref, wple_ref, gple_ref, wpg_ref, gfin_ref,
                out_ref, *, final):
    h = h_ref[0]
    u = _rmsnorm(h, gn_ref[0]).astype(BF16)
    dot = functools.partial(jnp.dot, preferred_element_type=F32)
    za = dot(u, wg_ref[0, :, 0:A_WIDTH])
    ya = dot((oa_ref[0] * (za * jax.nn.sigmoid(za))).astype(BF16), wa_ref[0])

    l1, l2, l3 = l1_ref[0], l2_ref[0], l3_ref[0]
    m = jnp.maximum(jnp.maximum(l1, l2), l3)
    e1, e2, e3 = jnp.exp(l1 - m), jnp.exp(l2 - m), jnp.exp(l3 - m)
    ob = (e1 * o1_ref[0] + e2 * o2_ref[0] + e3 * o3_ref[0]) / (e1 + e2 + e3)
    zb = dot(u, wg_ref[0, :, A_WIDTH:A_WIDTH + B_WIDTH])
    yb = dot((ob * (zb * jax.nn.sigmoid(zb))).astype(BF16), wb_ref[0])

    g0 = A_WIDTH + B_WIDTH
    ga = dot(u, wg_ref[0, :, g0:g0 + D_MODEL])
    gb = dot(u, wg_ref[0, :, g0 + D_MODEL:g0 + 2 * D_MODEL])
    merged = jax.nn.sigmoid(ga) * ya + jax.nn.sigmoid(gb) * yb
    h1 = h + dot(merged.astype(BF16), wo_ref[0])

    e = dot(p_ref[0, 0].astype(BF16), wple_ref[0])
    gate = jax.nn.sigmoid(dot(_rmsnorm(h1, gple_ref[0]).astype(BF16), wpg_ref[0]))
    h2 = h1 + gate * e
    if final:
        h2 = _rmsnorm(h2, gfin_ref[...])
    out_ref[0] = h2


def _out_call(h, oa, obs, lses, p, layer, wts, final):
    bsz, s_len, _ = h.shape
    tm = TM_OUT
    tok = lambda n: pl.BlockSpec((1, tm, n), lambda b, i: (b, i, 0))
    vec = pl.BlockSpec((1, 1, D_MODEL), lambda b, i: (layer, 0, 0))
    mat = lambda r, c: pl.BlockSpec((1, r, c), lambda b, i: (layer, 0, 0),
                                    pipeline_mode=pl.Buffered(1))
    in_specs = [tok(D_MODEL), tok(A_WIDTH)] + [tok(B_WIDTH)] * 6 + [
        pl.BlockSpec((1, 1, tm, PLE_DIM), lambda b, i: (layer, b, i, 0)),
        vec, mat(D_MODEL, GATE_WIDTH), mat(A_WIDTH, D_MODEL), mat(B_WIDTH, D_MODEL),
        mat(D_MODEL, D_MODEL), mat(PLE_DIM, D_MODEL), vec, mat(D_MODEL, D_MODEL),
        pl.BlockSpec((1, D_MODEL), lambda b, i: (0, 0)),
    ]
    return pl.pallas_call(
        functools.partial(_out_kernel, final=final),
        grid=(bsz, s_len // tm),
        in_specs=in_specs,
        out_specs=tok(D_MODEL),
        out_shape=jax.ShapeDtypeStruct((bsz, s_len, D_MODEL), F32),
        compiler_params=pltpu.CompilerParams(
            dimension_semantics=("parallel", "parallel"), vmem_limit_bytes=VMEM_LIMIT),
        name="out",
    )(h, oa, *obs, *lses, p, wts["g_norm"], wts["w_gate"], wts["w_a"], wts["w_b"], wts["w_o"],
      wts["w_ple"], wts["g_ple"], wts["w_pg"], wts["g_final"])


def _rope_tables(s_len):
    def table(pos, dim):
        freqs = ROPE_THETA ** (-jnp.arange(0, dim, 2, dtype=F32) / dim)
        ang = pos.astype(F32)[:, None] * freqs[None, :]
        ang = jnp.concatenate([ang, ang], axis=-1)
        return jnp.cos(ang), jnp.sin(ang)

    t = jnp.arange(s_len)
    half = HEAD_DIM // 2
    cos_r, sin_r = table(t // GRID_W, half)
    cos_c, sin_c = table(t % GRID_W, half)
    cos_a = jnp.concatenate([cos_r, cos_c], axis=-1)
    sin_a = jnp.concatenate([sin_r, sin_c], axis=-1)
    first_a = (jnp.arange(HEAD_DIM) % half) < half // 2
    cos_b, sin_b = table(t, HEAD_DIM)
    first_b = jnp.arange(HEAD_DIM) < half
    two = lambda x: jnp.concatenate([x, x], axis=-1)
    return tuple(two(x) for x in (
        cos_a, jnp.where(first_a, -sin_a, 0.0), jnp.where(first_a, 0.0, sin_a),
        cos_b, jnp.where(first_b, -sin_b, 0.0), jnp.where(first_b, 0.0, sin_b)))


def _prep_weights(g_norm, w_in, g_q, g_k, w_a, w_b, w_o, w_ple, g_ple, w_pg, g_final):
    w_qkv = jnp.concatenate([w_in[:, :, _OFF_QA:_OFF_ZA], w_in[:, :, _OFF_QB:_OFF_ZB]],
                            axis=-1).astype(BF16)
    w_gate = jnp.concatenate([w_in[:, :, _OFF_ZA:_OFF_QB], w_in[:, :, _OFF_ZB:_IN_WIDTH]],
                             axis=-1).astype(BF16)
    head = jnp.arange(LANES) // HEAD_DIM
    seg = jnp.where(head[:, None] == head[None, :], 1.0 / HEAD_DIM, 0.0)
    return dict(
        w_qkv=w_qkv, w_gate=w_gate, g_norm=g_norm[:, None, :],
        gq=jnp.concatenate([g_q, g_q], axis=-1)[:, None, :],
        gk=jnp.concatenate([g_k, g_k], axis=-1)[:, None, :],
        seg=jnp.concatenate([seg, seg], axis=0).astype(BF16),
        w_a=w_a.astype(BF16), w_b=w_b.astype(BF16), w_o=w_o.astype(BF16),
        w_ple=w_ple.astype(BF16), g_ple=g_ple[:, None, :], w_pg=w_pg.astype(BF16),
        g_final=g_final.reshape(1, D_MODEL))


def _trunk(x, p, wts, tabs):
    h = x
    for layer in range(DEPTH):
        qa, kd, vt, qb, kb, vb = _qkv_call(h, layer, wts["w_qkv"], wts["g_norm"], wts["gq"],
                                           wts["gk"], wts["seg"], tabs)
        oa = _attn_a_call(qa, kd, vt)
        obs, lses = zip(*[_attn_b_call(qb, kb, vb, g) for g in range(len(B_GROUPS))])
        h = _out_call(h, oa, obs, lses, p, layer, wts, final=(layer == DEPTH - 1))
    return h


def kernel(x_prompt, x_sample, p_prompt, p_sample, g_norm, w_in, g_q, g_k, w_a, w_b, w_o,
           w_ple, g_ple, w_pg, g_final):
    wts = _prep_weights(g_norm, w_in, g_q, g_k, w_a, w_b, w_o, w_ple, g_ple, w_pg, g_final)
    assert x_prompt.shape[1] == x_sample.shape[1]
    tabs = _rope_tables(x_prompt.shape[1])
    return (_trunk(x_prompt, p_prompt, wts, tabs), _trunk(x_sample, p_sample, wts, tabs))
```

```python
import functools
import math

import jax
import jax.numpy as jnp
from jax import lax
from jax.experimental import pallas as pl
from jax.experimental.pallas import tpu as pltpu

D_MODEL = 1024
DEPTH = 4
HEAD_DIM = 64
A_Q_HEADS = 8
A_KV_HEADS = 2
A_GROUP = A_Q_HEADS // A_KV_HEADS
B_GROUPS = ((128, 1), (512, 4), (2048, 16))
B_HEADS_PER_GROUP = 4
A_WIDTH = A_Q_HEADS * HEAD_DIM
A_KV_WIDTH = A_KV_HEADS * HEAD_DIM
B_WIDTH = B_HEADS_PER_GROUP * HEAD_DIM
B_QKV_WIDTH = B_WIDTH * len(B_GROUPS)
PLE_DIM = 256
GRID_W = 64
ROPE_THETA = 10000.0
EPS = 1e-6
SCALE = 1.0 / math.sqrt(HEAD_DIM)
LOG2E = math.log2(math.e)

_OFF_QA, _OFF_KA, _OFF_VA, _OFF_ZA = 0, 512, 640, 768
_OFF_QB, _OFF_KB, _OFF_VB, _OFF_ZB = 1280, 2048, 2816, 3584
_OFF_GA, _OFF_GB, _IN_WIDTH = 3840, 4864, 5888
QKV_WIDTH = A_WIDTH + 2 * A_KV_WIDTH + 3 * B_QKV_WIDTH
GATE_WIDTH = A_WIDTH + B_WIDTH + 2 * D_MODEL

LANES = 128
VMEM_LIMIT = 56 * 1024 * 1024

TM_QKV = 512
TQ_A = 128
TK_A = 4096
UNROLL_A = 1
ONES_ROWS = 16
MAX_FIXED_SHIFT = 60.0
TU_B = 256
SUB_B = 128
B_SIDE = 64
TM_OUT = 256

F32 = jnp.float32
BF16 = jnp.bfloat16


def _rmsnorm(x, g):
    return x * lax.rsqrt(jnp.mean(x * x, axis=-1, keepdims=True) + EPS) * g


def _rope(x, c, s_up, s_dn, half):
    return x * c + pltpu.roll(x, LANES - half, 1) * s_up + pltpu.roll(x, half, 1) * s_dn


def _head_meansq(x, seg):
    x2 = x * x
    hi = x2.astype(BF16)
    lo = (x2 - hi.astype(F32)).astype(BF16)
    return jnp.dot(jnp.concatenate([hi, lo], axis=-1), seg, preferred_element_type=F32)


def _qkv_kernel(h_ref, gn_ref, w_ref, gq_ref, gk_ref, seg_ref,
                ca_ref, sa1_ref, sa2_ref, cb_ref, sb1_ref, sb2_ref,
                qa_ref, kd_ref, vt_ref, *b_refs_and_scratch):
    n_grp = len(B_GROUPS)
    qb_refs, kb_refs, vb_refs = (b_refs_and_scratch[i * n_grp:(i + 1) * n_grp] for i in range(3))
    slab_ref = b_refs_and_scratch[3 * n_grp]
    tm = h_ref.shape[1]
    slabs = iter(range(slab_ref.shape[0]))

    def store_b(x, refs):
        refs[0][0] = x[:, 0:B_WIDTH].astype(BF16)
        for g in range(1, n_grp):
            dil = B_GROUPS[g][1]
            for blk in range(B_WIDTH // LANES):
                c0 = g * B_WIDTH + blk * LANES
                slab = slab_ref.at[next(slabs)]
                slab[...] = x[:, c0:c0 + LANES]
                for r in range(dil):
                    d0 = r * B_WIDTH + blk * LANES
                    refs[g][0, :, d0:d0 + LANES] = (
                        slab[pl.ds(r, tm // dil, stride=dil), :].astype(BF16))

    u = _rmsnorm(h_ref[0], gn_ref[0]).astype(BF16)
    seg = seg_ref[...]
    ca, sa1, sa2 = ca_ref[...], sa1_ref[...], sa2_ref[...]
    cb, sb1, sb2 = cb_ref[...], sb1_ref[...], sb2_ref[...]
    lane = lax.broadcasted_iota(jnp.int32, ca.shape, 1)

    def norm_rope_a(x, g):
        xn = x * lax.rsqrt(_head_meansq(x, seg) + EPS) * g
        return _rope(xn, ca, sa1, sa2, HEAD_DIM // 4)

    def rope_b(x):
        return [_rope(x[:, blk * LANES:(blk + 1) * LANES], cb, sb1, sb2, HEAD_DIM // 2)
                for blk in range(B_QKV_WIDTH // LANES)]

    qa = jnp.dot(u, w_ref[0, :, 0:A_WIDTH], preferred_element_type=F32)
    for blk in range(A_WIDTH // LANES):
        sl = slice(blk * LANES, (blk + 1) * LANES)
        qa_ref[0, :, sl] = (norm_rope_a(qa[:, sl], gq_ref[0]) * (SCALE * LOG2E)).astype(BF16)
    kv = jnp.dot(u, w_ref[0, :, A_WIDTH:A_WIDTH + 2 * A_KV_WIDTH], preferred_element_type=F32)
    ka = norm_rope_a(kv[:, 0:LANES], gk_ref[0])
    ka_sw = pltpu.roll(ka, HEAD_DIM, 1)
    kd_ref[0, 0] = jnp.where(lane < HEAD_DIM, ka, ka_sw).astype(BF16)
    kd_ref[0, 1] = jnp.where(lane < HEAD_DIM, ka_sw, ka).astype(BF16)
    vt_ref[0] = kv[:, LANES:2 * LANES].T.astype(BF16)

    base = A_WIDTH + 2 * A_KV_WIDTH
    qb = jnp.dot(u, w_ref[0, :, base:base + B_QKV_WIDTH], preferred_element_type=F32)
    store_b(jnp.concatenate([x * SCALE for x in rope_b(qb)], axis=1), qb_refs)
    kb = jnp.dot(u, w_ref[0, :, base + B_QKV_WIDTH:base + 2 * B_QKV_WIDTH],
                 preferred_element_type=F32)
    store_b(jnp.concatenate(rope_b(kb), axis=1), kb_refs)
    store_b(jnp.dot(u, w_ref[0, :, base + 2 * B_QKV_WIDTH:base + 3 * B_QKV_WIDTH],
                    preferred_element_type=F32), vb_refs)


def _qkv_call(h, layer, w_qkv, g_norm, gq, gk, seg, tabs):
    bsz, s_len, _ = h.shape
    tm = TM_QKV
    nt = s_len // tm
    n_grp = len(B_GROUPS)
    row = lambda b, i: (0, 0)
    tab_spec = pl.BlockSpec((tm, LANES), lambda b, i: (i, 0))
    vec_spec = lambda n: pl.BlockSpec((1, 1, n), lambda b, i: (layer, 0, 0))
    in_specs = [
        pl.BlockSpec((1, tm, D_MODEL), lambda b, i: (b, i, 0)),
        vec_spec(D_MODEL),
        pl.BlockSpec((1, D_MODEL, QKV_WIDTH), lambda b, i: (layer, 0, 0)),
        vec_spec(LANES), vec_spec(LANES),
        pl.BlockSpec((2 * LANES, LANES), row),
    ] + [tab_spec] * 6
    out_shape = [
        jax.ShapeDtypeStruct((bsz, s_len, A_WIDTH), BF16),
        jax.ShapeDtypeStruct((bsz, A_KV_HEADS, s_len, LANES), BF16),
        jax.ShapeDtypeStruct((bsz, A_KV_WIDTH, s_len), BF16),
    ]
    out_specs = [
        pl.BlockSpec((1, tm, A_WIDTH), lambda b, i: (b, i, 0)),
        pl.BlockSpec((1, A_KV_HEADS, tm, LANES), lambda b, i: (b, 0, i, 0)),
        pl.BlockSpec((1, A_KV_WIDTH, tm), lambda b, i: (b, 0, i)),
    ]
    for _ in range(3):
        for _, dil in B_GROUPS:
            out_shape.append(jax.ShapeDtypeStruct((bsz, s_len // dil, dil * B_WIDTH), BF16))
            out_specs.append(pl.BlockSpec((1, tm // dil, dil * B_WIDTH), lambda b, i: (b, i, 0)))
    n_slabs = 3 * (n_grp - 1) * (B_WIDTH // LANES)
    outs = pl.pallas_call(
        _qkv_kernel,
        grid=(bsz, nt),
        in_specs=in_specs,
        out_specs=out_specs,
        out_shape=out_shape,
        scratch_shapes=[pltpu.VMEM((n_slabs, tm, LANES), F32)],
        compiler_params=pltpu.CompilerParams(
            dimension_semantics=("parallel", "parallel"), vmem_limit_bytes=VMEM_LIMIT),
        name="qkv",
    )(h, g_norm, w_qkv, gq, gk, seg, *tabs)
    qa, kd, vt = outs[:3]
    qbs, kbs, vbs = (outs[3 + i * n_grp:3 + (i + 1) * n_grp] for i in range(3))
    return qa, kd, vt, qbs, kbs, vbs


def _attn_a_kernel(bound_ref, q_ref, kd_ref, vt_ref, o_ref, *, s_len, bounded):
    tq, tk = TQ_A, TK_A
    nq = A_GROUP * tq
    lane = lax.broadcasted_iota(jnp.int32, (tq, LANES), 1)
    ones = jnp.ones((ONES_ROWS, tk), BF16)
    qsts = []
    for j in range(A_KV_HEADS):
        parts = []
        for p in range(A_GROUP // 2):
            c0 = (j * A_GROUP + 2 * p) * HEAD_DIM
            blk = q_ref[0, :, c0:c0 + LANES].astype(F32)
            parts.append(jnp.where(lane < HEAD_DIM, blk, 0.0).astype(BF16))
            parts.append(jnp.where(lane < HEAD_DIM, 0.0, blk).astype(BF16))
        qsts.append(jnp.concatenate(parts, axis=0))

    def body(c, carry):
        off = pl.multiple_of(c * tk, tk)
        new = []
        for j in range(A_KV_HEADS):
            m, acc = carry[j]
            kc = kd_ref[0, j, pl.ds(off, tk), :]
            st = lax.dot_general(kc, qsts[j], (((1,), (1,)), ((), ())),
                                 preferred_element_type=F32)
            vc = jnp.concatenate(
                [vt_ref[0, j * HEAD_DIM:(j + 1) * HEAD_DIM, pl.ds(off, tk)], ones], axis=0)
            if bounded:
                p_t = jnp.exp2(st - m).astype(BF16)
                new.append((m, acc + jnp.dot(vc, p_t, preferred_element_type=F32)))
            else:
                m_new = jnp.maximum(m, jnp.max(st, axis=0, keepdims=True))
                p_t = jnp.exp2(st - m_new).astype(BF16)
                new.append((m_new, jnp.exp2(m - m_new) * acc
                            + jnp.dot(vc, p_t, preferred_element_type=F32)))
        return tuple(new)

    m0 = bound_ref[...] if bounded else jnp.full((1, nq), -jnp.inf, F32)
    init = tuple((m0, jnp.zeros((HEAD_DIM + ONES_ROWS, nq), F32)) for _ in range(A_KV_HEADS))
    res = lax.fori_loop(0, s_len // tk, body, init, unroll=UNROLL_A)
    for j in range(A_KV_HEADS):
        acc = res[j][1]
        o_t = acc[0:HEAD_DIM] / acc[HEAD_DIM:HEAD_DIM + 1]
        for p in range(A_GROUP // 2):
            pair = jnp.concatenate(
                [o_t[:, (2 * p) * tq:(2 * p + 1) * tq].T,
                 o_t[:, (2 * p + 1) * tq:(2 * p + 2) * tq].T], axis=1)
            c0 = (j * A_GROUP + 2 * p) * HEAD_DIM
            o_ref[0, :, c0:c0 + LANES] = pair


def _attn_a_call(qa, kd, vt, bound):
    bsz, s_len, _ = qa.shape

    def run(bounded):
        return pl.pallas_call(
            functools.partial(_attn_a_kernel, s_len=s_len, bounded=bounded),
            grid=(bsz, s_len // TQ_A),
            in_specs=[
                pl.BlockSpec((1, 1), lambda b, i: (0, 0)),
                pl.BlockSpec((1, TQ_A, A_WIDTH), lambda b, i: (b, i, 0)),
                pl.BlockSpec((1, A_KV_HEADS, s_len, LANES), lambda b, i: (b, 0, 0, 0)),
                pl.BlockSpec((1, A_KV_WIDTH, s_len), lambda b, i: (b, 0, 0)),
            ],
            out_specs=pl.BlockSpec((1, TQ_A, A_WIDTH), lambda b, i: (b, i, 0)),
            out_shape=jax.ShapeDtypeStruct((bsz, s_len, A_WIDTH), F32),
            compiler_params=pltpu.CompilerParams(
                dimension_semantics=("parallel", "arbitrary"), vmem_limit_bytes=VMEM_LIMIT),
            name="attn_a_bounded" if bounded else "attn_a_online",
        )(bound, qa, kd, vt)

    return lax.cond(bound[0, 0] <= MAX_FIXED_SHIFT, lambda: run(True), lambda: run(False))


def _attn_b_kernel(q_ref, k_ref, v_ref, o_ref, lse_ref, *, s_class):
    tu, sub = q_ref.shape[1], SUB_B
    win = sub + 2 * B_SIDE
    low = lax.broadcasted_iota(jnp.int32, (sub, LANES), 1) < HEAD_DIM
    for sb in range(tu // sub):
        rows = slice(sb * sub, (sb + 1) * sub)
        u0 = pl.program_id(2) * tu + sb * sub
        start = pl.multiple_of(jnp.clip(u0 - B_SIDE, 0, s_class - win), B_SIDE)
        qpos = u0 + (lax.broadcasted_iota(jnp.int32, (2 * sub, win), 0) & (sub - 1))
        kpos = start + lax.broadcasted_iota(jnp.int32, (2 * sub, win), 1)
        valid = jnp.abs(kpos - qpos) <= B_SIDE
        for p in range(B_HEADS_PER_GROUP // 2):
            sl = slice(p * LANES, (p + 1) * LANES)
            qblk = q_ref[0, rows, sl].astype(F32)
            qst = jnp.concatenate([jnp.where(low, qblk, 0.0), jnp.where(low, 0.0, qblk)],
                                  axis=0).astype(BF16)
            kp = k_ref[0, pl.ds(start, win), sl]
            vp = v_ref[0, pl.ds(start, win), sl]
            s = lax.dot_general(qst, kp, (((1,), (1,)), ((), ())), preferred_element_type=F32)
            s = jnp.where(valid, s, -jnp.inf)
            m = jnp.max(s, axis=-1, keepdims=True)
            e = jnp.exp(s - m)
            l = jnp.sum(e, axis=-1, keepdims=True)
            o2 = jnp.dot(e.astype(BF16), vp, preferred_element_type=F32) / l
            lse2 = m + jnp.log(l)
            o_ref[0, rows, sl] = jnp.where(low, o2[0:sub], o2[sub:2 * sub])
            lse_ref[0, rows, sl] = jnp.where(low, lse2[0:sub], lse2[sub:2 * sub])


def _attn_b_call(q, k, v, group):
    _, dil = B_GROUPS[group]
    bsz, s_class, _ = q.shape
    q_spec = pl.BlockSpec((1, TU_B, B_WIDTH), lambda b, r, u: (b, u, r))
    kv_spec = pl.BlockSpec((1, s_class, B_WIDTH), lambda b, r, u: (b, 0, r))
    o_shape = jax.ShapeDtypeStruct((bsz, s_class, dil * B_WIDTH), F32)
    return pl.pallas_call(
        functools.partial(_attn_b_kernel, s_class=s_class),
        grid=(bsz, dil, s_class // TU_B),
        in_specs=[q_spec, kv_spec, kv_spec],
        out_specs=[q_spec, q_spec],
        out_shape=[o_shape, o_shape],
        compiler_params=pltpu.CompilerParams(
            dimension_semantics=("parallel", "parallel", "arbitrary"),
            vmem_limit_bytes=VMEM_LIMIT),
        name=f"attn_b{group}",
    )(q, k, v)


def _out_kernel(h_ref, oa_ref, o1_ref, o2_ref, o3_ref, l1_ref, l2_ref, l3_ref, p_ref,
                gn_ref, wg_ref, wa_ref, wb_ref, wo_ref, wple_ref, gple_ref, wpg_ref, gfin_ref,
                out_ref, slab_ref, *, final):
    tm = h_ref.shape[1]
    slabs = iter(range(slab_ref.shape[0]))

    def token_major(ref, group):
        dil = B_GROUPS[group][1]
        if dil == 1:
            return ref[0]
        cols = []
        for blk in range(B_WIDTH // LANES):
            slab = slab_ref.at[next(slabs)]
            for r in range(dil):
                d0 = r * B_WIDTH + blk * LANES
                slab[pl.ds(r, tm // dil, stride=dil), :] = ref[0, :, d0:d0 + LANES]
            cols.append(slab[...])
        return jnp.concatenate(cols, axis=1)

    h = h_ref[0]
    u = _rmsnorm(h, gn_ref[0]).astype(BF16)
    dot = functools.partial(jnp.dot, preferred_element_type=F32)
    za = dot(u, wg_ref[0, :, 0:A_WIDTH])
    ya = dot((oa_ref[0] * (za * jax.nn.sigmoid(za))).astype(BF16), wa_ref[0])

    l1, l2, l3 = (token_major(r, g) for g, r in enumerate((l1_ref, l2_ref, l3_ref)))
    o1, o2, o3 = (token_major(r, g) for g, r in enumerate((o1_ref, o2_ref, o3_ref)))
    m = jnp.maximum(jnp.maximum(l1, l2), l3)
    e1, e2, e3 = jnp.exp(l1 - m), jnp.exp(l2 - m), jnp.exp(l3 - m)
    ob = (e1 * o1 + e2 * o2 + e3 * o3) / (e1 + e2 + e3)
    zb = dot(u, wg_ref[0, :, A_WIDTH:A_WIDTH + B_WIDTH])
    yb = dot((ob * (zb * jax.nn.sigmoid(zb))).astype(BF16), wb_ref[0])

    g0 = A_WIDTH + B_WIDTH
    ga = dot(u, wg_ref[0, :, g0:g0 + D_MODEL])
    gb = dot(u, wg_ref[0, :, g0 + D_MODEL:g0 + 2 * D_MODEL])
    merged = jax.nn.sigmoid(ga) * ya + jax.nn.sigmoid(gb) * yb
    h1 = h + dot(merged.astype(BF16), wo_ref[0])

    e = dot(p_ref[0, 0].astype(BF16), wple_ref[0])
    gate = jax.nn.sigmoid(dot(_rmsnorm(h1, gple_ref[0]).astype(BF16), wpg_ref[0]))
    h2 = h1 + gate * e
    if final:
        h2 = _rmsnorm(h2, gfin_ref[...])
    out_ref[0] = h2


def _out_call(h, oa, obs, lses, p, layer, wts, final):
    bsz, s_len, _ = h.shape
    tm = TM_OUT
    tok = lambda n: pl.BlockSpec((1, tm, n), lambda b, i: (b, i, 0))
    vec = pl.BlockSpec((1, 1, D_MODEL), lambda b, i: (layer, 0, 0))
    mat = lambda r, c: pl.BlockSpec((1, r, c), lambda b, i: (layer, 0, 0),
                                    pipeline_mode=pl.Buffered(1))
    cls = [pl.BlockSpec((1, tm // dil, dil * B_WIDTH), lambda b, i: (b, i, 0))
           for _, dil in B_GROUPS]
    n_slabs = 2 * (len(B_GROUPS) - 1) * (B_WIDTH // LANES)
    in_specs = [tok(D_MODEL), tok(A_WIDTH)] + cls + cls + [
        pl.BlockSpec((1, 1, tm, PLE_DIM), lambda b, i: (layer, b, i, 0)),
        vec, mat(D_MODEL, GATE_WIDTH), mat(A_WIDTH, D_MODEL), mat(B_WIDTH, D_MODEL),
        mat(D_MODEL, D_MODEL), mat(PLE_DIM, D_MODEL), vec, mat(D_MODEL, D_MODEL),
        pl.BlockSpec((1, D_MODEL), lambda b, i: (0, 0)),
    ]
    return pl.pallas_call(
        functools.partial(_out_kernel, final=final),
        grid=(bsz, s_len // tm),
        in_specs=in_specs,
        out_specs=tok(D_MODEL),
        out_shape=jax.ShapeDtypeStruct((bsz, s_len, D_MODEL), F32),
        scratch_shapes=[pltpu.VMEM((n_slabs, tm, LANES), F32)],
        compiler_params=pltpu.CompilerParams(
            dimension_semantics=("parallel", "parallel"), vmem_limit_bytes=VMEM_LIMIT),
        name="out",
    )(h, oa, *obs, *lses, p, wts["g_norm"], wts["w_gate"], wts["w_a"], wts["w_b"], wts["w_o"],
      wts["w_ple"], wts["g_ple"], wts["w_pg"], wts["g_final"])


def _rope_tables(s_len):
    def table(pos, dim):
        freqs = ROPE_THETA ** (-jnp.arange(0, dim, 2, dtype=F32) / dim)
        ang = pos.astype(F32)[:, None] * freqs[None, :]
        ang = jnp.concatenate([ang, ang], axis=-1)
        return jnp.cos(ang), jnp.sin(ang)

    t = jnp.arange(s_len)
    half = HEAD_DIM // 2
    cos_r, sin_r = table(t // GRID_W, half)
    cos_c, sin_c = table(t % GRID_W, half)
    cos_a = jnp.concatenate([cos_r, cos_c], axis=-1)
    sin_a = jnp.concatenate([sin_r, sin_c], axis=-1)
    first_a = (jnp.arange(HEAD_DIM) % half) < half // 2
    cos_b, sin_b = table(t, HEAD_DIM)
    first_b = jnp.arange(HEAD_DIM) < half
    two = lambda x: jnp.concatenate([x, x], axis=-1)
    return tuple(two(x) for x in (
        cos_a, jnp.where(first_a, -sin_a, 0.0), jnp.where(first_a, 0.0, sin_a),
        cos_b, jnp.where(first_b, -sin_b, 0.0), jnp.where(first_b, 0.0, sin_b)))


def _prep_weights(g_norm, w_in, g_q, g_k, w_a, w_b, w_o, w_ple, g_ple, w_pg, g_final):
    w_qkv = jnp.concatenate([w_in[:, :, _OFF_QA:_OFF_ZA], w_in[:, :, _OFF_QB:_OFF_ZB]],
                            axis=-1).astype(BF16)
    w_gate = jnp.concatenate([w_in[:, :, _OFF_ZA:_OFF_QB], w_in[:, :, _OFF_ZB:_IN_WIDTH]],
                             axis=-1).astype(BF16)
    head = jnp.arange(LANES) // HEAD_DIM
    seg = jnp.where(head[:, None] == head[None, :], 1.0 / HEAD_DIM, 0.0)
    score_bound = (jnp.max(jnp.abs(g_q), axis=-1) * jnp.max(jnp.abs(g_k), axis=-1)
                   * (HEAD_DIM * SCALE * LOG2E * (1.0 + 2.0 ** -6)))
    return dict(
        score_bound=score_bound.reshape(DEPTH, 1, 1),
        w_qkv=w_qkv, w_gate=w_gate, g_norm=g_norm[:, None, :],
        gq=jnp.concatenate([g_q, g_q], axis=-1)[:, None, :],
        gk=jnp.concatenate([g_k, g_k], axis=-1)[:, None, :],
        seg=jnp.concatenate([seg, seg], axis=0).astype(BF16),
        w_a=w_a.astype(BF16), w_b=w_b.astype(BF16), w_o=w_o.astype(BF16),
        w_ple=w_ple.astype(BF16), g_ple=g_ple[:, None, :], w_pg=w_pg.astype(BF16),
        g_final=g_final.reshape(1, D_MODEL))


def _trunk(x, p, wts, tabs):
    h = x
    for layer in range(DEPTH):
        qa, kd, vt, qbs, kbs, vbs = _qkv_call(h, layer, wts["w_qkv"], wts["g_norm"], wts["gq"],
                                              wts["gk"], wts["seg"], tabs)
        oa = _attn_a_call(qa, kd, vt, wts["score_bound"][layer])
        obs, lses = zip(*[_attn_b_call(qbs[g], kbs[g], vbs[g], g) for g in range(len(B_GROUPS))])
        h = _out_call(h, oa, obs, lses, p, layer, wts, final=(layer == DEPTH - 1))
    return h


def kernel(x_prompt, x_sample, p_prompt, p_sample, g_norm, w_in, g_q, g_k, w_a, w_b, w_o,
           w_ple, g_ple, w_pg, g_final):
    wts = _prep_weights(g_norm, w_in, g_q, g_k, w_a, w_b, w_o, w_ple, g_ple, w_pg, g_final)
    assert x_prompt.shape[1] == x_sample.shape[1]
    tabs = _rope_tables(x_prompt.shape[1])
    return (_trunk(x_prompt, p_prompt, wts, tabs), _trunk(x_sample, p_sample, wts, tabs))
```
